```python
import math
import jax, jax.numpy as jnp
from jax import lax
import numpy as np

D_MODEL = 1024
BATCH = 8
SEQ = 2048
DEPTH = 1

HEAD_DIM = 64
FOX_HEADS = D_MODEL // 128
DIFF_HEADS = D_MODEL // 256
FOX_WIDTH = FOX_HEADS * HEAD_DIM
DIFF_WIDTH = DIFF_HEADS * 2 * HEAD_DIM
MIX_WIDTH = FOX_WIDTH + DIFF_WIDTH
IN_COLS = 3 * FOX_WIDTH + FOX_HEADS + 3 * DIFF_WIDTH
ROPE_THETA = 500000.0
ROT_DIM = HEAD_DIM // 4
Q_BLOCK = 128
PEER_HEADS = 8
PEER_KEYS = 128
PEER_EXPERTS = PEER_KEYS * PEER_KEYS
PEER_KEY_DIM = 128
PEER_HALF = PEER_KEY_DIM // 2
PEER_TOPK = 16
TOKEN_CHUNK = 128
NORM_EPS = 1e-6
SUBLN_EPS = 1e-5

kernel_name = "hybrid_fox_diffattn_peer_adaln"


def _rmsnorm(x, g, eps):
    xf = x.astype(jnp.float32)
    y = xf * lax.rsqrt(jnp.mean(xf * xf, axis=-1, keepdims=True) + eps)
    return (y * g.astype(jnp.float32)).astype(x.dtype)


def _partial_rope(t, pos):
    half = ROT_DIM // 2
    inv = ROPE_THETA ** (-jnp.arange(0, ROT_DIM, 2, dtype=jnp.float32) / ROT_DIM)
    ang = pos[:, None] * inv[None, :]
    cos, sin = jnp.cos(ang), jnp.sin(ang)
    tr = t[..., :ROT_DIM].astype(jnp.float32)
    t1, t2 = tr[..., :half], tr[..., half:]
    rot = jnp.concatenate([t1 * cos - t2 * sin, t2 * cos + t1 * sin], axis=-1)
    return jnp.concatenate([rot.astype(t.dtype), t[..., ROT_DIM:]], axis=-1)


def _fox_attention(q, k, v, logf):
    B, H, S, dh = q.shape
    nb = S // Q_BLOCK
    F = jnp.cumsum(logf, axis=-1)
    qb = q.reshape(B, H, nb, Q_BLOCK, dh).transpose(2, 0, 1, 3, 4)
    Fb = F.reshape(B, H, nb, Q_BLOCK).transpose(2, 0, 1, 3)
    kf = k.astype(jnp.float32)
    kpos = jnp.arange(S)
    scale = dh ** -0.5

    def block(args):
        qi, Fi, bi = args
        s = jnp.einsum('bhqd,bhkd->bhqk', qi.astype(jnp.float32), kf) * scale
        s = s + Fi[..., None] - F[:, :, None, :]
        qpos = bi * Q_BLOCK + jnp.arange(Q_BLOCK)
        s = jnp.where(kpos[None, :] <= qpos[:, None], s, -jnp.inf)
        p = jax.nn.softmax(s, axis=-1)
        return jnp.einsum('bhqk,bhkd->bhqd', p.astype(v.dtype), v)

    o = lax.map(block, (qb, Fb, jnp.arange(nb)))
    return o.transpose(1, 2, 0, 3, 4).reshape(B, H, S, dh)


def _diff_attention(q, k, v, lam):
    B, H, _, S, dh = q.shape
    nb = S // Q_BLOCK
    qb = q.reshape(B, H, 2, nb, Q_BLOCK, dh).transpose(3, 0, 1, 2, 4, 5)
    kf = k.astype(jnp.float32)
    kpos = jnp.arange(S)
    scale = dh ** -0.5

    def block(args):
        qi, bi = args
        s = jnp.einsum('bhcqd,bhckd->bhcqk', qi.astype(jnp.float32), kf) * scale
        qpos = bi * Q_BLOCK + jnp.arange(Q_BLOCK)
        s = jnp.where(kpos[None, :] <= qpos[:, None], s, -jnp.inf)
        p = jax.nn.softmax(s, axis=-1)
        pd = p[:, :, 0] - lam * p[:, :, 1]
        return jnp.einsum('bhqk,bhkd->bhqd', pd.astype(v.dtype), v)

    o = lax.map(block, (qb, jnp.arange(nb)))
    return o.transpose(1, 2, 0, 3, 4).reshape(B, H, S, 2 * dh)


def _peer(h, w_pq, sub_keys, u_tab, v_tab):
    B, S, D = h.shape
    q = (h @ w_pq).reshape(B, S, PEER_HEADS, 2, PEER_HALF).astype(jnp.float32)
    sk = sub_keys.astype(jnp.float32)
    s1 = jnp.einsum('bshd,nd->bshn', q[..., 0, :], sk[0])
    s2 = jnp.einsum('bshd,nd->bshn', q[..., 1, :], sk[1])
    sc1, i1 = lax.top_k(s1, PEER_TOPK)
    sc2, i2 = lax.top_k(s2, PEER_TOPK)
    comb = (sc1[..., :, None] + sc2[..., None, :]).reshape(B, S, PEER_HEADS, PEER_TOPK * PEER_TOPK)
    top, ci = lax.top_k(comb, PEER_TOPK)
    e_idx = (jnp.take_along_axis(i1, ci // PEER_TOPK, axis=-1) * PEER_KEYS
             + jnp.take_along_axis(i2, ci % PEER_TOPK, axis=-1))
    g = jax.nn.softmax(top, axis=-1)
    T = B * S
    nc = T // TOKEN_CHUNK
    K = PEER_HEADS * PEER_TOPK
    hc = h.reshape(nc, TOKEN_CHUNK, D)
    ic = e_idx.reshape(nc, TOKEN_CHUNK, K)
    gc = g.reshape(nc, TOKEN_CHUNK, K)

    def chunk(args):
        hx, ix, gx = args
        u = u_tab[ix]
        a = jnp.einsum('ckd,cd->ck', u.astype(jnp.float32), hx.astype(jnp.float32))
        w = jax.nn.gelu(a, approximate=False) * gx
        return jnp.einsum('ck,ckd->cd', w.astype(h.dtype), v_tab[ix])

    out = lax.map(chunk, (hc, ic, gc))
    return out.reshape(B, S, D)


def setup_inputs(seed: int = 0) -> dict:
    key = jax.random.key(seed)
    ks = jax.random.split(key, 20)
    f32 = jnp.float32
    nrm = lambda k, shape, s: (jax.random.normal(k, shape, f32) * s)
    return {
        "x": nrm(ks[0], (BATCH, SEQ, D_MODEL), 1.0),
        "c": nrm(ks[1], (BATCH, D_MODEL), 1.0),
        "w_ada": nrm(ks[2], (DEPTH, D_MODEL, 6 * D_MODEL), 0.5 * D_MODEL ** -0.5),
        "b_ada": nrm(ks[3], (DEPTH, 6 * D_MODEL), 0.02),
        "g_attn": 1.0 + nrm(ks[4], (DEPTH, D_MODEL), 0.02),
        "w_in": nrm(ks[5], (DEPTH, D_MODEL, IN_COLS), D_MODEL ** -0.5),
        "b_f": nrm(ks[6], (DEPTH, FOX_HEADS), 0.1),
        "lambda_q1": nrm(ks[7], (DEPTH, HEAD_DIM), 0.1),
        "lambda_k1": nrm(ks[8], (DEPTH, HEAD_DIM), 0.1),
        "lambda_q2": nrm(ks[9], (DEPTH, HEAD_DIM), 0.1),
        "lambda_k2": nrm(ks[10], (DEPTH, HEAD_DIM), 0.1),
        "g_subln": 1.0 + nrm(ks[11], (DEPTH, 2 * HEAD_DIM), 0.02),
        "w_o": nrm(ks[12], (DEPTH, MIX_WIDTH, D_MODEL), MIX_WIDTH ** -0.5),
        "g_ffn": 1.0 + nrm(ks[13], (DEPTH, D_MODEL), 0.02),
        "w_pq": nrm(ks[14], (DEPTH, D_MODEL, PEER_HEADS * PEER_KEY_DIM), D_MODEL ** -0.5),
        "sub_keys": nrm(ks[15], (DEPTH, 2, PEER_KEYS, PEER_HALF), PEER_HALF ** -0.5),
        "u_experts": nrm(ks[16], (DEPTH, PEER_EXPERTS, D_MODEL), D_MODEL ** -0.5),
        "v_experts": nrm(ks[17], (DEPTH, PEER_EXPERTS, D_MODEL), PEER_TOPK ** -0.5),
        "g_final": 1.0 + nrm(ks[18], (D_MODEL,), 0.02),
    }


def reference(x, c, w_ada, b_ada, g_attn, w_in, b_f, lambda_q1, lambda_k1, lambda_q2,
              lambda_k2, g_subln, w_o, g_ffn, w_pq, sub_keys, u_experts, v_experts, g_final):
    B, S, D = x.shape
    pos = jnp.arange(S, dtype=jnp.float32)
    split_at = np.cumsum([FOX_WIDTH, FOX_WIDTH, FOX_WIDTH, FOX_HEADS,
                          DIFF_WIDTH, DIFF_WIDTH]).tolist()
    for l in range(DEPTH):
        mod = (jax.nn.silu(c.astype(jnp.float32)) @ w_ada[l].astype(jnp.float32)
               + b_ada[l].astype(jnp.float32)).astype(x.dtype)
        sh1, sc1, gt1, sh2, sc2, gt2 = [m[:, None, :] for m in jnp.split(mod, 6, axis=-1)]

        h = _rmsnorm(x, g_attn[l], NORM_EPS) * (1 + sc1) + sh1
        proj = h @ w_in[l]
        fq, fk, fv, fg, dq, dk, dv = jnp.split(proj, split_at, axis=-1)

        to_heads = lambda t, H, dh: t.reshape(B, S, H, dh).transpose(0, 2, 1, 3)
        fq, fk, fv = (to_heads(t, FOX_HEADS, HEAD_DIM) for t in (fq, fk, fv))
        logf = jax.nn.log_sigmoid((fg + b_f[l]).astype(jnp.float32)).transpose(0, 2, 1)
        o_fox = _fox_attention(fq, fk, fv, logf)

        to_pairs = lambda t: t.reshape(B, S, DIFF_HEADS, 2, HEAD_DIM).transpose(0, 2, 3, 1, 4)
        dq = _partial_rope(to_pairs(dq), pos)
        dk = _partial_rope(to_pairs(dk), pos)
        dv = to_heads(dv, DIFF_HEADS, 2 * HEAD_DIM)
        lam_init = 0.8 - 0.6 * math.exp(-0.3 * l)
        lam = (jnp.exp(jnp.sum(lambda_q1[l].astype(jnp.float32) * lambda_k1[l].astype(jnp.float32)))
               - jnp.exp(jnp.sum(lambda_q2[l].astype(jnp.float32) * lambda_k2[l].astype(jnp.float32)))
               + lam_init)
        o_diff = _diff_attention(dq, dk, dv, lam)
        o_diff = _rmsnorm(o_diff, g_subln[l], SUBLN_EPS) * (1.0 - lam_init)

        mixed = jnp.concatenate([
            o_fox.transpose(0, 2, 1, 3).reshape(B, S, FOX_WIDTH),
            o_diff.transpose(0, 2, 1, 3).reshape(B, S, DIFF_WIDTH).astype(o_fox.dtype)], axis=-1)
        x = x + gt1 * (mixed @ w_o[l])

        h2 = _rmsnorm(x, g_ffn[l], NORM_EPS) * (1 + sc2) + sh2
        x = x + gt2 * _peer(h2, w_pq[l], sub_keys[l], u_experts[l], v_experts[l])
    return _rmsnorm(x, g_final, NORM_EPS)
```

```python
import functools
import math

import numpy as np
import jax
import jax.numpy as jnp
from jax import lax
from jax.experimental import pallas as pl
from jax.experimental.pallas import tpu as pltpu

F32 = jnp.float32
BF16 = jnp.bfloat16

HEAD_DIM = 64
ROT_DIM = HEAD_DIM // 4
ROPE_THETA = 500000.0
NORM_EPS = 1e-6
SUBLN_EPS = 1e-5
LANES = 128
PEER_HEADS = 8
PEER_KEYS = 128
PEER_HALF = 64
PEER_TOPK = 16
LAM_INIT = 0.8 - 0.6 * math.exp(-0.3 * 0)

VMEM_LIMIT = 56 * 1024 * 1024

LN_TILE = 256
ATT_TILE = 256
ROUTE_TILE = 256
ROUTE_CHUNK = 128
PEER_TOK = 512
PEER_EXP = 1024

_CAND_GROUPS = ((0, 0, 8), (0, 8, 8), (1, 0, 8), (2, 0, 5), (3, 0, 4), (4, 0, 3),
                (5, 0, 2), (6, 0, 2), (7, 0, 2))
_N_CAND_ROWS = 8 * (len(_CAND_GROUPS) + 1)


def _candidate_tables():
    idx = np.zeros((_N_CAND_ROWS, LANES), np.float32)
    neg = np.zeros((_N_CAND_ROWS, LANES), np.float32)
    for g, (r1, b0, cnt) in enumerate(_CAND_GROUPS):
        for w in range(8):
            idx[g * 8 + w] = r1 * PEER_TOPK + b0 + w
            neg[g * 8 + w] = 0.0 if w < cnt else -np.inf
    g = len(_CAND_GROUPS)
    for w in range(8):
        idx[g * 8 + w] = (8 + w) * PEER_TOPK
    return jnp.asarray(idx), jnp.asarray(neg)


def _nt_dot(a, b, precision=None):
    return lax.dot_general(a, b, (((1,), (1,)), ((), ())), precision=precision,
                           preferred_element_type=F32)


def _adaln_kernel(c_ref, w_ref, b_ref, o_ref):
    c = c_ref[...]
    s = c / (1.0 + jnp.exp(-c))
    o_ref[...] = jnp.dot(s, w_ref[...], precision=lax.Precision.HIGHEST,
                         preferred_element_type=F32) + b_ref[...]


def _adaln(c, w, b):
    bsz, d = c.shape
    n = w.shape[1]
    return pl.pallas_call(
        _adaln_kernel,
        out_shape=jax.ShapeDtypeStruct((bsz, n), F32),
        grid=(n // d,),
        in_specs=[pl.BlockSpec((bsz, d), lambda j: (0, 0)),
                  pl.BlockSpec((d, d), lambda j: (0, j)),
                  pl.BlockSpec((1, d), lambda j: (0, j))],
        out_specs=pl.BlockSpec((bsz, d), lambda j: (0, j)),
        compiler_params=pltpu.CompilerParams(dimension_semantics=("arbitrary",),
                                             vmem_limit_bytes=VMEM_LIMIT),
        name="adaln",
    )(c, w, b.reshape(1, n))


def _ln_inproj_kernel(x_ref, mod_ref, g_ref, w_ref, wfg_ref, bf_ref, rc_ref, ra_ref, rb_ref,
                      q_ref, k_ref, v_ref, f_ref, carry_ref):
    i = pl.program_id(1)
    d = x_ref.shape[1]
    half = d // 2

    @pl.when(i == 0)
    def _():
        carry_ref[...] = jnp.zeros_like(carry_ref)

    x = x_ref[...]
    ms = jnp.mean(x * x, axis=-1, keepdims=True)
    y = x * lax.rsqrt(ms + NORM_EPS) * g_ref[...]
    h = y * (1.0 + mod_ref[1:2, :]) + mod_ref[0:1, :]
    proj = jnp.dot(h.astype(BF16), w_ref[...], preferred_element_type=F32)

    scale = HEAD_DIM ** -0.5
    q_ref[:, :half] = (proj[:, :half] * scale).astype(BF16)
    k_ref[:, :half] = proj[:, d:d + half].astype(BF16)
    v_ref[...] = proj[:, 2 * d:].astype(BF16)
    rc, ra, rb = rc_ref[...], ra_ref[...], rb_ref[...]
    for c in range(half // LANES):
        lo = half + c * LANES
        for base, ref, mul in ((0, q_ref, scale), (d, k_ref, 1.0)):
            t = proj[:, base + lo:base + lo + LANES]
            r = (t * rc + pltpu.roll(t, LANES - ROT_DIM // 2, 1) * ra
                 + pltpu.roll(t, ROT_DIM // 2, 1) * rb)
            ref[:, lo:lo + LANES] = (r * mul).astype(BF16)

    z = _nt_dot(wfg_ref[...], h, lax.Precision.HIGHEST) + bf_ref[:, 0:1]
    logf = jnp.minimum(z, 0.0) - jnp.log1p(jnp.exp(-jnp.abs(z)))
    t = x.shape[0]
    tri = (lax.broadcasted_iota(jnp.int32, (t, t), 0)
           <= lax.broadcasted_iota(jnp.int32, (t, t), 1)).astype(F32)
    f = jnp.dot(logf, tri, precision=lax.Precision.HIGHEST,
                preferred_element_type=F32) + carry_ref[:, 0:1]
    f_ref[...] = f
    carry_ref[...] = jnp.broadcast_to(f[:, t - 1:t], carry_ref.shape)


def _ln_inproj(x, mod, g, w_qkv, w_fg_t, b_f, rope_c, rope_a, rope_b):
    bsz, s, d = x.shape
    t = LN_TILE
    nh = w_fg_t.shape[0]
    act = jax.ShapeDtypeStruct((bsz, s, d), BF16)
    tok = pl.BlockSpec((None, t, d), lambda b, i: (b, i, 0))
    rope = pl.BlockSpec((t, LANES), lambda b, i: (i, 0))
    return pl.pallas_call(
        _ln_inproj_kernel,
        out_shape=(act, act, act, jax.ShapeDtypeStruct((bsz, nh, s), F32)),
        grid=(bsz, s // t),
        in_specs=[tok,
                  pl.BlockSpec((None, 6, d), lambda b, i: (b, 0, 0)),
                  pl.BlockSpec((1, d), lambda b, i: (0, 0)),
                  pl.BlockSpec((d, 3 * d), lambda b, i: (0, 0)),
                  pl.BlockSpec((nh, d), lambda b, i: (0, 0)),
                  pl.BlockSpec((nh, LANES), lambda b, i: (0, 0)),
                  rope, rope, rope],
        out_specs=(tok, tok, tok, pl.BlockSpec((None, nh, t), lambda b, i: (b, 0, i))),
        scratch_shapes=[pltpu.VMEM((nh, LANES), F32)],
        compiler_params=pltpu.CompilerParams(dimension_semantics=("arbitrary", "arbitrary"),
                                             vmem_limit_bytes=VMEM_LIMIT),
        name="ln_inproj",
    )(x, mod, g, w_qkv, w_fg_t, b_f, rope_c, rope_a, rope_b)


def _attn_kernel(q_ref, k_ref, v_ref, f_ref, lam_ref, gs_ref, o_ref):
    u = pl.program_id(1)
    qi = pl.program_id(2)
    tq = q_ref.shape[0]
    tk = ATT_TILE
    lane = lax.broadcasted_iota(jnp.int32, (tq, LANES), 1)
    q = q_ref[...]
    zero = jnp.zeros_like(q)
    q_sub = (jnp.where(lane < HEAD_DIM, q, zero), jnp.where(lane >= HEAD_DIM, q, zero))
    causal = (lax.broadcasted_iota(jnp.int32, (tq, tk), 1)
              <= lax.broadcasted_iota(jnp.int32, (tq, tk), 0))

    def flash(qm, bias_row):
        def step(ki, carry, diagonal):
            m, l, acc = carry
            ks = pl.multiple_of(ki * tk, tk)
            s = _nt_dot(qm, k_ref[pl.ds(ks, tk), :])
            if bias_row is not None:
                s = s - f_ref[bias_row:bias_row + 1, pl.ds(ks, tk)]
            if diagonal:
                s = jnp.where(causal, s, -jnp.inf)
            m_new = jnp.maximum(m, jnp.max(s, axis=1, keepdims=True))
            alpha = jnp.exp(m - m_new)
            p = jnp.exp(s - m_new)
            l = alpha * l + jnp.sum(p, axis=1, keepdims=True)
            acc = alpha * acc + jnp.dot(p.astype(BF16), v_ref[pl.ds(ks, tk), :],
                                        preferred_element_type=F32)
            return m_new, l, acc

        init = (jnp.full((tq, 1), -jnp.inf, F32), jnp.zeros((tq, 1), F32),
                jnp.zeros((tq, LANES), F32))
        carry = lax.fori_loop(0, qi, lambda ki, c: step(ki, c, False), init)
        _, l, acc = step(qi, carry, True)
        return acc / l

    @pl.when(u < 4)
    def _():
        o_lo = flash(q_sub[0], 0)
        o_hi = flash(q_sub[1], 1)
        o_ref[...] = jnp.where(lane < HEAD_DIM, o_lo, o_hi).astype(o_ref.dtype)

    @pl.when(u >= 4)
    def _():
        lam = (jnp.exp(jnp.sum(lam_ref[0:1, :] * lam_ref[1:2, :], axis=1, keepdims=True))
               - jnp.exp(jnp.sum(lam_ref[2:3, :] * lam_ref[3:4, :], axis=1, keepdims=True))
               + LAM_INIT)
        o = flash(q_sub[0], None) - lam * flash(q_sub[1], None)
        ms = jnp.mean(o * o, axis=-1, keepdims=True)
        o = o * lax.rsqrt(ms + SUBLN_EPS) * gs_ref[...] * (1.0 - LAM_INIT)
        o_ref[...] = o.astype(o_ref.dtype)


def _attention(q, k, v, f, lam, g_subln):
    bsz, s, d = q.shape
    nu = d // LANES
    tq = ATT_TILE
    nfp = f.shape[1]
    tile = pl.BlockSpec((None, tq, LANES), lambda b, u, i: (b, i, u))
    full = pl.BlockSpec((None, s, LANES), lambda b, u, i: (b, 0, u))
    return pl.pallas_call(
        _attn_kernel,
        out_shape=jax.ShapeDtypeStruct((bsz, s, d), BF16),
        grid=(bsz, nu, s // tq),
        in_specs=[tile, full, full,
                  pl.BlockSpec((None, None, 2, s),
                               lambda b, u, i: (b, jnp.minimum(u, nfp - 1), 0, 0)),
                  pl.BlockSpec((4, HEAD_DIM), lambda b, u, i: (0, 0)),
                  pl.BlockSpec((1, LANES), lambda b, u, i: (0, 0))],
        out_specs=tile,
        compiler_params=pltpu.CompilerParams(
            dimension_semantics=("arbitrary", "arbitrary", "arbitrary"),
            vmem_limit_bytes=VMEM_LIMIT),
        name="attention",
    )(q, k, v, f, lam, g_subln)


def _outproj_kernel(m_ref, x_ref, mod_ref, w_ref, g_ref, x1_ref, h2_ref):
    y = jnp.dot(m_ref[...], w_ref[...], preferred_element_type=F32)
    x1 = x_ref[...] + mod_ref[2:3, :] * y
    x1_ref[...] = x1
    ms = jnp.mean(x1 * x1, axis=-1, keepdims=True)
    h = x1 * lax.rsqrt(ms + NORM_EPS) * g_ref[...]
    h2_ref[...] = (h * (1.0 + mod_ref[4:5, :]) + mod_ref[3:4, :]).astype(h2_ref.dtype)


def _outproj(mixed, x, mod, w_o, g):
    bsz, s, d = x.shape
    t = LN_TILE
    tok = pl.BlockSpec((None, t, d), lambda b, i: (b, i, 0))
    return pl.pallas_call(
        _outproj_kernel,
        out_shape=(jax.ShapeDtypeStruct((bsz, s, d), F32), jax.ShapeDtypeStruct((bsz, s, d), BF16)),
        grid=(bsz, s // t),
        in_specs=[tok, tok,
                  pl.BlockSpec((None, 6, d), lambda b, i: (b, 0, 0)),
                  pl.BlockSpec((d, d), lambda b, i: (0, 0)),
                  pl.BlockSpec((1, d), lambda b, i: (0, 0))],
        out_specs=(tok, tok),
        compiler_params=pltpu.CompilerParams(dimension_semantics=("arbitrary", "arbitrary"),
                                             vmem_limit_bytes=VMEM_LIMIT),
        name="outproj",
    )(mixed, x, mod, w_o, g)


def _top16(x, kidx):
    cur = x
    rank = jnp.full(x.shape, float(PEER_TOPK), F32)
    vals = []
    for r in range(PEER_TOPK):
        m = jnp.max(cur, axis=0, keepdims=True)
        first = jnp.min(jnp.where(cur == m, kidx, float(PEER_KEYS)), axis=0, keepdims=True)
        sel = kidx == first
        rank = jnp.where(sel, float(r), rank)
        cur = jnp.where(sel, -jnp.inf, cur)
        vals.append(m)
    return vals, rank


def _route_kernel(h_ref, w_ref, sk_ref, cidx_ref, cneg_ref, r2_ref, e2_ref, n1_ref, c1_ref, qt_ref):
    qt_ref[...] = _nt_dot(w_ref[...], h_ref[...])
    nchunk = h_ref.shape[0] // ROUTE_CHUNK
    kidx = lax.broadcasted_iota(jnp.int32, (PEER_KEYS, ROUTE_CHUNK), 0).astype(F32)
    cidx = cidx_ref[...]
    cneg = cneg_ref[...]
    sk1 = sk_ref[0]
    sk2 = sk_ref[1]

    def head(hd, _):
        base = pl.multiple_of(hd * PEER_KEYS, PEER_KEYS)
        q1 = qt_ref[pl.ds(base, PEER_HALF), :].astype(BF16)
        q2 = qt_ref[pl.ds(base + PEER_HALF, PEER_HALF), :].astype(BF16)
        s1_all = jnp.dot(sk1, q1, preferred_element_type=F32)
        s2_all = jnp.dot(sk2, q2, preferred_element_type=F32)
        for c in range(nchunk):
            cols = slice(c * ROUTE_CHUNK, (c + 1) * ROUTE_CHUNK)
            s1 = s1_all[:, cols]
            s2 = s2_all[:, cols]
            a, rank1 = _top16(s1, kidx)
            b, rank2 = _top16(s2, kidx)
            b_lo = jnp.concatenate(b[:8], axis=0)
            b_hi = jnp.concatenate(b[8:], axis=0)
            groups = [a[r1] + (b_lo if b0 == 0 else b_hi) for r1, b0, _ in _CAND_GROUPS]
            groups.append(jnp.concatenate(a[8:], axis=0) + b[0])
            cand = jnp.concatenate(groups, axis=0) + cneg
            cur = cand
            chosen = jnp.zeros(cand.shape, F32)
            for _ in range(PEER_TOPK):
                m = jnp.max(cur, axis=0, keepdims=True)
                first = jnp.min(jnp.where(cur == m, cidx, 1e9), axis=0, keepdims=True)
                sel = cidx == first
                chosen = jnp.where(sel, 1.0, chosen)
                cur = jnp.where(sel, -jnp.inf, cur)
            rowsum = lambda lo, hi: jnp.sum(chosen[lo:hi], axis=0, keepdims=True)
            counts = [rowsum(0, 16)] + [rowsum(8 * g, 8 * g + 8) for g in range(2, 9)]
            counts += [chosen[72 + w:73 + w] for w in range(8)]
            z = jnp.sum(jnp.where(chosen > 0.0, jnp.exp(cand - (a[0] + b[0])), 0.0),
                        axis=0, keepdims=True)
            n1 = jnp.zeros(s1.shape, F32)
            for r in range(PEER_TOPK):
                n1 = jnp.where(rank1 == float(r), counts[r], n1)
            r2_ref[hd, :, cols] = rank2
            e2_ref[hd, :, cols] = jnp.exp(s2 - b[0])
            n1_ref[hd, :, cols] = n1
            c1_ref[hd, :, cols] = jnp.exp(s1 - a[0]) / z
        return 0

    lax.fori_loop(0, PEER_HEADS, head, 0)


def _route(h2, w_pq_t, sub_keys, cidx, cneg):
    n, d = h2.shape
    t = ROUTE_TILE
    nq = w_pq_t.shape[0]
    tab = jax.ShapeDtypeStruct((PEER_HEADS, PEER_KEYS, n), F32)
    tab_spec = pl.BlockSpec((PEER_HEADS, PEER_KEYS, t), lambda i: (0, 0, i))
    const = pl.BlockSpec((_N_CAND_ROWS, LANES), lambda i: (0, 0))
    return pl.pallas_call(
        _route_kernel,
        out_shape=(tab, tab, tab, tab),
        grid=(n // t,),
        in_specs=[pl.BlockSpec((t, d), lambda i: (i, 0)),
                  pl.BlockSpec((nq, d), lambda i: (0, 0)),
                  pl.BlockSpec((2, PEER_KEYS, PEER_HALF), lambda i: (0, 0, 0)),
                  const, const],
        out_specs=(tab_spec, tab_spec, tab_spec, tab_spec),
        scratch_shapes=[pltpu.VMEM((nq, t), F32)],
        compiler_params=pltpu.CompilerParams(dimension_semantics=("arbitrary",),
                                             vmem_limit_bytes=VMEM_LIMIT),
        name="peer_route",
    )(h2, w_pq_t, sub_keys, cidx, cneg)


def _peer_kernel(h_ref, u_ref, vt_ref, r2_ref, e2_ref, n1_ref, c1_ref, x1_ref, mod_ref, g_ref,
                 o_ref, acc_ref, a_ref, w_ref):
    e = pl.program_id(1)
    tt = h_ref.shape[0]
    nblk = u_ref.shape[0] // PEER_KEYS

    @pl.when(e == 0)
    def _():
        acc_ref[...] = jnp.zeros_like(acc_ref)

    a_ref[...] = _nt_dot(u_ref[...], h_ref[...])

    def block(jj, _):
        row = pl.multiple_of(jj * PEER_KEYS, PEER_KEYS)
        n1_blk = n1_ref[jj]
        c1_blk = c1_ref[jj]
        for c in range(tt // LANES):
            cols = slice(c * LANES, (c + 1) * LANES)
            gate = jnp.zeros((PEER_KEYS, LANES), F32)
            for hd in range(PEER_HEADS):
                n1 = n1_blk[hd:hd + 1, cols]
                c1 = c1_blk[hd:hd + 1, cols]
                gate = gate + jnp.where(r2_ref[hd, :, cols] < n1, e2_ref[hd, :, cols] * c1, 0.0)
            a = a_ref[pl.ds(row, PEER_KEYS), cols]
            act = 0.5 * a * (1.0 + lax.erf(a * math.sqrt(0.5)))
            w_ref[pl.ds(row, PEER_KEYS), cols] = (act * gate).astype(w_ref.dtype)
        return 0

    lax.fori_loop(0, nblk, block, 0)
    acc_ref[...] += jnp.dot(vt_ref[...], w_ref[...], preferred_element_type=F32)

    @pl.when(e == pl.num_programs(1) - 1)
    def _():
        x = x1_ref[...] + mod_ref[5:6, :] * acc_ref[...].T
        ms = jnp.mean(x * x, axis=-1, keepdims=True)
        o_ref[...] = x * lax.rsqrt(ms + NORM_EPS) * g_ref[...]


def _peer(h2, u, v_t, r2, e2, n1, c1, x1, mod, g_final, seq):
    n, d = h2.shape
    ne = u.shape[0]
    tt, eb = PEER_TOK, PEER_EXP
    tab = pl.BlockSpec((PEER_HEADS, PEER_KEYS, tt), lambda t, e: (0, 0, t))
    tok = pl.BlockSpec((tt, d), lambda t, e: (t, 0))
    row = pl.BlockSpec((eb // PEER_KEYS, PEER_HEADS, tt), lambda t, e: (e, 0, t))
    return pl.pallas_call(
        _peer_kernel,
        out_shape=jax.ShapeDtypeStruct((n, d), F32),
        grid=(n // tt, ne // eb),
        in_specs=[tok,
                  pl.BlockSpec((eb, d), lambda t, e: (e, 0)),
                  pl.BlockSpec((d, eb), lambda t, e: (0, e)),
                  tab, tab, row, row,
                  tok,
                  pl.BlockSpec((None, 6, d), lambda t, e: ((t * tt) // seq, 0, 0)),
                  pl.BlockSpec((1, d), lambda t, e: (0, 0))],
        out_specs=tok,
        scratch_shapes=[pltpu.VMEM((d, tt), F32), pltpu.VMEM((eb, tt), F32),
                        pltpu.VMEM((eb, tt), BF16)],
        compiler_params=pltpu.CompilerParams(dimension_semantics=("arbitrary", "arbitrary"),
                                             vmem_limit_bytes=VMEM_LIMIT),
        name="peer_experts",
    )(h2, u, v_t, r2, e2, n1, c1, x1, mod, g_final)


def _rope_tables(seq):
    half = ROT_DIM // 2
    inv = ROPE_THETA ** (-jnp.arange(0, ROT_DIM, 2, dtype=F32) / ROT_DIM)
    ang = jnp.arange(seq, dtype=F32)[:, None] * inv[None, :]
    cos, sin = jnp.cos(ang), jnp.sin(ang)
    ones = jnp.ones((seq, HEAD_DIM - ROT_DIM), F32)
    zeros = jnp.zeros((seq, HEAD_DIM - ROT_DIM), F32)
    zh = jnp.zeros((seq, half), F32)
    per_head = lambda parts: jnp.tile(jnp.concatenate(parts, axis=1), (1, LANES // HEAD_DIM))
    return (per_head([cos, cos, ones]), per_head([-sin, zh, zeros]), per_head([zh, sin, zeros]))


def kernel(x, c, w_ada, b_ada, g_attn, w_in, b_f, lambda_q1, lambda_k1, lambda_q2, lambda_k2,
           g_subln, w_o, g_ffn, w_pq, sub_keys, u_experts, v_experts, g_final):
    bsz, seq, d = x.shape
    assert w_ada.shape[0] == 1 and d == 8 * LANES
    assert seq % ATT_TILE == 0 and seq % PEER_TOK == 0 and seq % LN_TILE == 0
    fox_w = d // 2
    nfh = fox_w // HEAD_DIM

    mod = _adaln(c, w_ada[0], b_ada[0]).reshape(bsz, 6, d)

    wi = w_in[0]
    o = 3 * fox_w + nfh
    w_qkv = jnp.concatenate([wi[:, :fox_w], wi[:, o:o + fox_w],
                             wi[:, fox_w:2 * fox_w], wi[:, o + fox_w:o + 2 * fox_w],
                             wi[:, 2 * fox_w:3 * fox_w], wi[:, o + 2 * fox_w:]], axis=1).astype(BF16)
    w_fg_t = wi[:, 3 * fox_w:o].T
    b_fb = jnp.broadcast_to(b_f[0][:, None], (nfh, LANES))
    rope_c, rope_a, rope_b = _rope_tables(seq)

    q, k, v, f = _ln_inproj(x, mod, g_attn, w_qkv, w_fg_t, b_fb, rope_c, rope_a, rope_b)

    lam = jnp.concatenate([lambda_q1, lambda_k1, lambda_q2, lambda_k2], axis=0)
    mixed = _attention(q, k, v, f.reshape(bsz, nfh // 2, 2, seq), lam, g_subln)

    x1, h2 = _outproj(mixed, x, mod, w_o[0].astype(BF16), g_ffn)

    n = bsz * seq
    h2 = h2.reshape(n, d)
    cidx, cneg = _candidate_tables()
    r2, e2, n1, c1 = _route(h2, w_pq[0].T.astype(BF16), sub_keys[0].astype(BF16), cidx, cneg)

    n1 = n1.transpose(1, 0, 2)
    c1 = c1.transpose(1, 0, 2)
    out = _peer(h2, u_experts[0].astype(BF16), v_experts[0].T.astype(BF16), r2, e2, n1, c1,
                x1.reshape(n, d), mod, g_final.reshape(1, d), seq)
    return out.reshape(bsz, seq, d)
```

```python
import functools
import math

import numpy as np
import jax
import jax.numpy as jnp
from jax import lax
from jax.experimental import pallas as pl
from jax.experimental.pallas import tpu as pltpu

F32 = jnp.float32
BF16 = jnp.bfloat16

HEAD_DIM = 64
ROT_DIM = HEAD_DIM // 4
ROPE_THETA = 500000.0
NORM_EPS = 1e-6
SUBLN_EPS = 1e-5
LANES = 128
PEER_HEADS = 8
PEER_KEYS = 128
PEER_HALF = 64
PEER_TOPK = 16
LAM_INIT = 0.8 - 0.6 * math.exp(-0.3 * 0)

VMEM_LIMIT = 56 * 1024 * 1024

LN_TILE = 256
ATT_TILE = 512
ROUTE_TILE = 256
ROUTE_CHUNK = 128
PEER_TOK = 512
PEER_EXP = 1024
GATE_ROWS = 16

_CAND_GROUPS = ((0, 0, 8), (0, 8, 8), (1, 0, 8), (2, 0, 5), (3, 0, 4), (4, 0, 3),
                (5, 0, 2), (6, 0, 2), (7, 0, 2))
_N_CAND_ROWS = 8 * (len(_CAND_GROUPS) + 1)


def _candidate_tables():
    idx = np.zeros((_N_CAND_ROWS, LANES), np.float32)
    neg = np.zeros((_N_CAND_ROWS, LANES), np.float32)
    for g, (r1, b0, cnt) in enumerate(_CAND_GROUPS):
        for w in range(8):
            idx[g * 8 + w] = r1 * PEER_TOPK + b0 + w
            neg[g * 8 + w] = 0.0 if w < cnt else -np.inf
    g = len(_CAND_GROUPS)
    for w in range(8):
        idx[g * 8 + w] = (8 + w) * PEER_TOPK
    return jnp.asarray(idx), jnp.asarray(neg)


def _nt_dot(a, b, precision=None):
    return lax.dot_general(a, b, (((1,), (1,)), ((), ())), precision=precision,
                           preferred_element_type=F32)


def _adaln_kernel(c_ref, w_ref, b_ref, o_ref):
    c = c_ref[...]
    s = c / (1.0 + jnp.exp(-c))
    o_ref[...] = jnp.dot(s, w_ref[...], precision=lax.Precision.HIGHEST,
                         preferred_element_type=F32) + b_ref[...]


def _adaln(c, w, b):
    bsz, d = c.shape
    n = w.shape[1]
    return pl.pallas_call(
        _adaln_kernel,
        out_shape=jax.ShapeDtypeStruct((bsz, n), F32),
        grid=(n // d,),
        in_specs=[pl.BlockSpec((bsz, d), lambda j: (0, 0)),
                  pl.BlockSpec((d, d), lambda j: (0, j)),
                  pl.BlockSpec((1, d), lambda j: (0, j))],
        out_specs=pl.BlockSpec((bsz, d), lambda j: (0, j)),
        compiler_params=pltpu.CompilerParams(dimension_semantics=("arbitrary",),
                                             vmem_limit_bytes=VMEM_LIMIT),
        name="adaln",
    )(c, w, b.reshape(1, n))


def _ln_inproj_kernel(x_ref, mod_ref, g_ref, w_ref, wfg_ref, bf_ref, rc_ref, ra_ref, rb_ref,
                      q_ref, k_ref, v_ref, f_ref, carry_ref):
    i = pl.program_id(1)
    d = x_ref.shape[1]
    half = d // 2

    @pl.when(i == 0)
    def _():
        carry_ref[...] = jnp.zeros_like(carry_ref)

    x = x_ref[...]
    ms = jnp.mean(x * x, axis=-1, keepdims=True)
    y = x * lax.rsqrt(ms + NORM_EPS) * g_ref[...]
    h = y * (1.0 + mod_ref[1:2, :]) + mod_ref[0:1, :]
    proj = jnp.dot(h.astype(BF16), w_ref[...], preferred_element_type=F32)

    scale = HEAD_DIM ** -0.5
    q_ref[:, :half] = (proj[:, :half] * scale).astype(BF16)
    k_ref[:, :half] = proj[:, d:d + half].astype(BF16)
    v_ref[...] = proj[:, 2 * d:].astype(BF16)
    rc, ra, rb = rc_ref[...], ra_ref[...], rb_ref[...]
    for c in range(half // LANES):
        lo = half + c * LANES
        for base, ref, mul in ((0, q_ref, scale), (d, k_ref, 1.0)):
            t = proj[:, base + lo:base + lo + LANES]
            r = (t * rc + pltpu.roll(t, LANES - ROT_DIM // 2, 1) * ra
                 + pltpu.roll(t, ROT_DIM // 2, 1) * rb)
            ref[:, lo:lo + LANES] = (r * mul).astype(BF16)

    z = _nt_dot(wfg_ref[...], h, lax.Precision.HIGHEST) + bf_ref[:, 0:1]
    logf = jnp.minimum(z, 0.0) - jnp.log1p(jnp.exp(-jnp.abs(z)))
    t = x.shape[0]
    tri = (lax.broadcasted_iota(jnp.int32, (t, t), 0)
           <= lax.broadcasted_iota(jnp.int32, (t, t), 1)).astype(F32)
    f = jnp.dot(logf, tri, precision=lax.Precision.HIGHEST,
                preferred_element_type=F32) + carry_ref[:, 0:1]
    f_ref[...] = f
    carry_ref[...] = jnp.broadcast_to(f[:, t - 1:t], carry_ref.shape)


def _ln_inproj(x, mod, g, w_qkv, w_fg_t, b_f, rope_c, rope_a, rope_b):
    bsz, s, d = x.shape
    t = LN_TILE
    nh = w_fg_t.shape[0]
    act = jax.ShapeDtypeStruct((bsz, s, d), BF16)
    tok = pl.BlockSpec((None, t, d), lambda b, i: (b, i, 0))
    rope = pl.BlockSpec((t, LANES), lambda b, i: (i, 0))
    return pl.pallas_call(
        _ln_inproj_kernel,
        out_shape=(act, act, act, jax.ShapeDtypeStruct((bsz, nh, s), F32)),
        grid=(bsz, s // t),
        in_specs=[tok,
                  pl.BlockSpec((None, 6, d), lambda b, i: (b, 0, 0)),
                  pl.BlockSpec((1, d), lambda b, i: (0, 0)),
                  pl.BlockSpec((d, 3 * d), lambda b, i: (0, 0)),
                  pl.BlockSpec((nh, d), lambda b, i: (0, 0)),
                  pl.BlockSpec((nh, LANES), lambda b, i: (0, 0)),
                  rope, rope, rope],
        out_specs=(tok, tok, tok, pl.BlockSpec((None, nh, t), lambda b, i: (b, 0, i))),
        scratch_shapes=[pltpu.VMEM((nh, LANES), F32)],
        compiler_params=pltpu.CompilerParams(dimension_semantics=("arbitrary", "arbitrary"),
                                             vmem_limit_bytes=VMEM_LIMIT),
        name="ln_inproj",
    )(x, mod, g, w_qkv, w_fg_t, b_f, rope_c, rope_a, rope_b)


def _attn_kernel(q_ref, k_ref, v_ref, f_ref, lam_ref, gs_ref, o_ref):
    u = pl.program_id(1)
    qi = pl.program_id(2)
    tq = q_ref.shape[0]
    tk = ATT_TILE
    lane = lax.broadcasted_iota(jnp.int32, (tq, LANES), 1)
    q = q_ref[...]
    zero = jnp.zeros_like(q)
    q_sub = (jnp.where(lane < HEAD_DIM, q, zero), jnp.where(lane >= HEAD_DIM, q, zero))
    causal = (lax.broadcasted_iota(jnp.int32, (tq, tk), 1)
              <= lax.broadcasted_iota(jnp.int32, (tq, tk), 0))

    def flash(forget_bias):
        def step(ki, carry, diagonal):
            ks = pl.multiple_of(ki * tk, tk)
            kb = k_ref[pl.ds(ks, tk), :]
            vb = v_ref[pl.ds(ks, tk), :]
            out = []
            for c in range(2):
                m, l, acc = carry[c]
                s = _nt_dot(q_sub[c], kb)
                if forget_bias:
                    s = s - f_ref[c:c + 1, pl.ds(ks, tk)]
                if diagonal:
                    s = jnp.where(causal, s, -jnp.inf)
                m_new = jnp.maximum(m, jnp.max(s, axis=1, keepdims=True))
                alpha = jnp.exp(m - m_new)
                p = jnp.exp(s - m_new)
                l = alpha * l + jnp.sum(p, axis=1, keepdims=True)
                acc = alpha * acc + jnp.dot(p.astype(BF16), vb, preferred_element_type=F32)
                out.append((m_new, l, acc))
            return tuple(out)

        init = (jnp.full((tq, 1), -jnp.inf, F32), jnp.zeros((tq, 1), F32),
                jnp.zeros((tq, LANES), F32))
        carry = lax.fori_loop(0, qi, lambda ki, c: step(ki, c, False), (init, init))
        carry = step(qi, carry, True)
        return tuple(acc / l for _, l, acc in carry)

    @pl.when(u < 4)
    def _():
        o_lo, o_hi = flash(True)
        o_ref[...] = jnp.where(lane < HEAD_DIM, o_lo, o_hi).astype(o_ref.dtype)

    @pl.when(u >= 4)
    def _():
        lam = (jnp.exp(jnp.sum(lam_ref[0:1, :] * lam_ref[1:2, :], axis=1, keepdims=True))
               - jnp.exp(jnp.sum(lam_ref[2:3, :] * lam_ref[3:4, :], axis=1, keepdims=True))
               + LAM_INIT)
        o0, o1 = flash(False)
        o = o0 - lam * o1
        ms = jnp.mean(o * o, axis=-1, keepdims=True)
        o = o * lax.rsqrt(ms + SUBLN_EPS) * gs_ref[...] * (1.0 - LAM_INIT)
        o_ref[...] = o.astype(o_ref.dtype)


def _attention(q, k, v, f, lam, g_subln):
    bsz, s, d = q.shape
    nu = d // LANES
    tq = ATT_TILE
    nfp = f.shape[1]
    tile = pl.BlockSpec((None, tq, LANES), lambda b, u, i: (b, i, u))
    full = pl.BlockSpec((None, s, LANES), lambda b, u, i: (b, 0, u))
    return pl.pallas_call(
        _attn_kernel,
        out_shape=jax.ShapeDtypeStruct((bsz, s, d), BF16),
        grid=(bsz, nu, s // tq),
        in_specs=[tile, full, full,
                  pl.BlockSpec((None, None, 2, s),
                               lambda b, u, i: (b, jnp.minimum(u, nfp - 1), 0, 0)),
                  pl.BlockSpec((4, HEAD_DIM), lambda b, u, i: (0, 0)),
                  pl.BlockSpec((1, LANES), lambda b, u, i: (0, 0))],
        out_specs=tile,
        compiler_params=pltpu.CompilerParams(
            dimension_semantics=("arbitrary", "arbitrary", "arbitrary"),
            vmem_limit_bytes=VMEM_LIMIT),
        name="attention",
    )(q, k, v, f, lam, g_subln)


def _outproj_kernel(m_ref, x_ref, mod_ref, w_ref, g_ref, x1_ref, h2_ref):
    y = jnp.dot(m_ref[...], w_ref[...], preferred_element_type=F32)
    x1 = x_ref[...] + mod_ref[2:3, :] * y
    x1_ref[...] = x1
    ms = jnp.mean(x1 * x1, axis=-1, keepdims=True)
    h = x1 * lax.rsqrt(ms + NORM_EPS) * g_ref[...]
    h2_ref[...] = (h * (1.0 + mod_ref[4:5, :]) + mod_ref[3:4, :]).astype(h2_ref.dtype)


def _outproj(mixed, x, mod, w_o, g):
    bsz, s, d = x.shape
    t = LN_TILE
    tok = pl.BlockSpec((None, t, d), lambda b, i: (b, i, 0))
    return pl.pallas_call(
        _outproj_kernel,
        out_shape=(jax.ShapeDtypeStruct((bsz, s, d), F32), jax.ShapeDtypeStruct((bsz, s, d), BF16)),
        grid=(bsz, s // t),
        in_specs=[tok, tok,
                  pl.BlockSpec((None, 6, d), lambda b, i: (b, 0, 0)),
                  pl.BlockSpec((d, d), lambda b, i: (0, 0)),
                  pl.BlockSpec((1, d), lambda b, i: (0, 0))],
        out_specs=(tok, tok),
        compiler_params=pltpu.CompilerParams(dimension_semantics=("arbitrary", "arbitrary"),
                                             vmem_limit_bytes=VMEM_LIMIT),
        name="outproj",
    )(mixed, x, mod, w_o, g)


def _top16(x, kidx):
    cur = x
    rank = jnp.full(x.shape, float(PEER_TOPK), F32)
    vals = []
    for r in range(PEER_TOPK):
        m = jnp.max(cur, axis=0, keepdims=True)
        first = jnp.min(jnp.where(cur == m, kidx, float(PEER_KEYS)), axis=0, keepdims=True)
        sel = kidx == first
        rank = jnp.where(sel, float(r), rank)
        cur = jnp.where(sel, -jnp.inf, cur)
        vals.append(m)
    return vals, rank


def _route_kernel(h_ref, w_ref, sk_ref, cidx_ref, cneg_ref, r2_ref, e2_ref, n1_ref, c1_ref, qt_ref):
    qt_ref[...] = _nt_dot(w_ref[...], h_ref[...])
    nchunk = h_ref.shape[0] // ROUTE_CHUNK
    kidx = lax.broadcasted_iota(jnp.int32, (PEER_KEYS, ROUTE_CHUNK), 0).astype(F32)
    cidx = cidx_ref[...]
    cneg = cneg_ref[...]
    sk1 = sk_ref[0]
    sk2 = sk_ref[1]

    def head(hd, _):
        base = pl.multiple_of(hd * PEER_KEYS, PEER_KEYS)
        q1 = qt_ref[pl.ds(base, PEER_HALF), :].astype(BF16)
        q2 = qt_ref[pl.ds(base + PEER_HALF, PEER_HALF), :].astype(BF16)
        s1_all = jnp.dot(sk1, q1, preferred_element_type=F32)
        s2_all = jnp.dot(sk2, q2, preferred_element_type=F32)
        for c in range(nchunk):
            cols = slice(c * ROUTE_CHUNK, (c + 1) * ROUTE_CHUNK)
            s1 = s1_all[:, cols]
            s2 = s2_all[:, cols]
            a, rank1 = _top16(s1, kidx)
            b, rank2 = _top16(s2, kidx)
            b_lo = jnp.concatenate(b[:8], axis=0)
            b_hi = jnp.concatenate(b[8:], axis=0)
            groups = [a[r1] + (b_lo if b0 == 0 else b_hi) for r1, b0, _ in _CAND_GROUPS]
            groups.append(jnp.concatenate(a[8:], axis=0) + b[0])
            cand = jnp.concatenate(groups, axis=0) + cneg
            cur = cand
            chosen = jnp.zeros(cand.shape, F32)
            for _ in range(PEER_TOPK):
                m = jnp.max(cur, axis=0, keepdims=True)
                first = jnp.min(jnp.where(cur == m, cidx, 1e9), axis=0, keepdims=True)
                sel = cidx == first
                chosen = jnp.where(sel, 1.0, chosen)
                cur = jnp.where(sel, -jnp.inf, cur)
            rowsum = lambda lo, hi: jnp.sum(chosen[lo:hi], axis=0, keepdims=True)
            counts = [rowsum(0, 16)] + [rowsum(8 * g, 8 * g + 8) for g in range(2, 9)]
            counts += [chosen[72 + w:73 + w] for w in range(8)]
            z = jnp.sum(jnp.where(chosen > 0.0, jnp.exp(cand - (a[0] + b[0])), 0.0),
                        axis=0, keepdims=True)
            n1 = jnp.zeros(s1.shape, F32)
            for r in range(PEER_TOPK):
                n1 = jnp.where(rank1 == float(r), counts[r], n1)
            r2_ref[hd, c] = rank2.astype(r2_ref.dtype)
            e2_ref[hd, c] = jnp.exp(s2 - b[0]).astype(e2_ref.dtype)
            n1_ref[hd, :, cols] = n1
            c1_ref[hd, :, cols] = jnp.exp(s1 - a[0]) / z
        return 0

    lax.fori_loop(0, PEER_HEADS, head, 0)


def _route(h2, w_pq_t, sub_keys, cidx, cneg):
    n, d = h2.shape
    t = ROUTE_TILE
    nq = w_pq_t.shape[0]
    tab = jax.ShapeDtypeStruct((PEER_HEADS, PEER_KEYS, n), F32)
    tab_spec = pl.BlockSpec((PEER_HEADS, PEER_KEYS, t), lambda i: (0, 0, i))
    ctab = jax.ShapeDtypeStruct((PEER_HEADS, n // ROUTE_CHUNK, PEER_KEYS, ROUTE_CHUNK), BF16)
    ctab_spec = pl.BlockSpec((PEER_HEADS, t // ROUTE_CHUNK, PEER_KEYS, ROUTE_CHUNK),
                             lambda i: (0, i, 0, 0))
    const = pl.BlockSpec((_N_CAND_ROWS, LANES), lambda i: (0, 0))
    return pl.pallas_call(
        _route_kernel,
        out_shape=(ctab, ctab, tab, tab),
        grid=(n // t,),
        in_specs=[pl.BlockSpec((t, d), lambda i: (i, 0)),
                  pl.BlockSpec((nq, d), lambda i: (0, 0)),
                  pl.BlockSpec((2, PEER_KEYS, PEER_HALF), lambda i: (0, 0, 0)),
                  const, const],
        out_specs=(ctab_spec, ctab_spec, tab_spec, tab_spec),
        scratch_shapes=[pltpu.VMEM((nq, t), F32)],
        compiler_params=pltpu.CompilerParams(dimension_semantics=("arbitrary",),
                                             vmem_limit_bytes=VMEM_LIMIT),
        name="peer_route",
    )(h2, w_pq_t, sub_keys, cidx, cneg)


def _peer_kernel(h_ref, u_ref, vt_ref, r2_in, e2_in, n1_ref, c1_ref, x1_ref, mod_ref, g_ref,
                 o_ref, acc_ref, a_ref, w_ref, r2_ref, e2_ref):
    e = pl.program_id(1)
    tt = h_ref.shape[0]
    nblk = u_ref.shape[0] // PEER_KEYS
    nchunk = tt // LANES
    nstrip = PEER_KEYS // GATE_ROWS

    @pl.when(e == 0)
    def _():
        acc_ref[...] = jnp.zeros_like(acc_ref)
        r2_ref[...] = r2_in[...]
        e2_ref[...] = e2_in[...]

    a = _nt_dot(u_ref[...], h_ref[...])
    for c in range(nchunk):
        a_ref[c] = a[:, c * LANES:(c + 1) * LANES]

    def tile(i, _):
        jj = i // nchunk
        c = i % nchunk
        row = pl.multiple_of(jj * PEER_KEYS, PEER_KEYS)
        col = pl.multiple_of(c * LANES, LANES)
        n1_blk = n1_ref[jj, :, pl.ds(col, LANES)]
        c1_blk = c1_ref[jj, :, pl.ds(col, LANES)]
        gate = [None] * nstrip
        for hd in range(PEER_HEADS):
            n1 = jnp.broadcast_to(n1_blk[hd:hd + 1], (GATE_ROWS, LANES)).astype(BF16)
            c1 = jnp.broadcast_to(c1_blk[hd:hd + 1], (GATE_ROWS, LANES)).astype(BF16)
            for s in range(nstrip):
                keys = slice(s * GATE_ROWS, (s + 1) * GATE_ROWS)
                term = jnp.where(r2_ref[hd, c, keys, :] < n1, e2_ref[hd, c, keys, :] * c1, 0.0)
                gate[s] = term if gate[s] is None else gate[s] + term
        a_blk = a_ref[c, pl.ds(row, PEER_KEYS), :]
        act = (a_blk * (0.5 + 0.5 * lax.erf(a_blk * math.sqrt(0.5)))).astype(BF16)
        for s in range(nstrip):
            w_ref[c, pl.ds(row + s * GATE_ROWS, GATE_ROWS), :] = (
                act[s * GATE_ROWS:(s + 1) * GATE_ROWS] * gate[s])
        return 0

    lax.fori_loop(0, nblk * nchunk, tile, 0, unroll=4)
    w = jnp.concatenate([w_ref[c] for c in range(nchunk)], axis=1)
    acc_ref[...] += jnp.dot(vt_ref[...], w, preferred_element_type=F32)

    @pl.when(e == pl.num_programs(1) - 1)
    def _():
        x = x1_ref[...] + mod_ref[5:6, :] * acc_ref[...].T
        ms = jnp.mean(x * x, axis=-1, keepdims=True)
        o_ref[...] = x * lax.rsqrt(ms + NORM_EPS) * g_ref[...]


def _peer(h2, u, v_t, r2, e2, n1, c1, x1, mod, g_final, seq):
    n, d = h2.shape
    ne = u.shape[0]
    tt, eb = PEER_TOK, PEER_EXP
    tab = pl.BlockSpec((PEER_HEADS, tt // LANES, PEER_KEYS, LANES), lambda t, e: (0, t, 0, 0))
    tok = pl.BlockSpec((tt, d), lambda t, e: (t, 0))
    row = pl.BlockSpec((eb // PEER_KEYS, PEER_HEADS, tt), lambda t, e: (e, 0, t))
    return pl.pallas_call(
        _peer_kernel,
        out_shape=jax.ShapeDtypeStruct((n, d), F32),
        grid=(n // tt, ne // eb),
        in_specs=[tok,
                  pl.BlockSpec((eb, d), lambda t, e: (e, 0)),
                  pl.BlockSpec((d, eb), lambda t, e: (0, e)),
                  tab, tab, row, row,
                  tok,
                  pl.BlockSpec((None, 6, d), lambda t, e: ((t * tt) // seq, 0, 0)),
                  pl.BlockSpec((1, d), lambda t, e: (0, 0))],
        out_specs=tok,
        scratch_shapes=[pltpu.VMEM((d, tt), F32), pltpu.VMEM((tt // LANES, eb, LANES), F32),
                        pltpu.VMEM((tt // LANES, eb, LANES), BF16),
                        pltpu.VMEM((PEER_HEADS, tt // LANES, PEER_KEYS, LANES), BF16),
                        pltpu.VMEM((PEER_HEADS, tt // LANES, PEER_KEYS, LANES), BF16)],
        compiler_params=pltpu.CompilerParams(dimension_semantics=("arbitrary", "arbitrary"),
                                             vmem_limit_bytes=VMEM_LIMIT),
        name="peer_experts",
    )(h2, u, v_t, r2, e2, n1, c1, x1, mod, g_final)


def _rope_tables(seq):
    half = ROT_DIM // 2
    inv = ROPE_THETA ** (-jnp.arange(0, ROT_DIM, 2, dtype=F32) / ROT_DIM)
    ang = jnp.arange(seq, dtype=F32)[:, None] * inv[None, :]
    cos, sin = jnp.cos(ang), jnp.sin(ang)
    ones = jnp.ones((seq, HEAD_DIM - ROT_DIM), F32)
    zeros = jnp.zeros((seq, HEAD_DIM - ROT_DIM), F32)
    zh = jnp.zeros((seq, half), F32)
    per_head = lambda parts: jnp.tile(jnp.concatenate(parts, axis=1), (1, LANES // HEAD_DIM))
    return (per_head([cos, cos, ones]), per_head([-sin, zh, zeros]), per_head([zh, sin, zeros]))


def kernel(x, c, w_ada, b_ada, g_attn, w_in, b_f, lambda_q1, lambda_k1, lambda_q2, lambda_k2,
           g_subln, w_o, g_ffn, w_pq, sub_keys, u_experts, v_experts, g_final):
    bsz, seq, d = x.shape
    assert w_ada.shape[0] == 1 and d == 8 * LANES
    assert seq % ATT_TILE == 0 and seq % PEER_TOK == 0 and seq % LN_TILE == 0
    fox_w = d // 2
    nfh = fox_w // HEAD_DIM

    mod = _adaln(c, w_ada[0], b_ada[0]).reshape(bsz, 6, d)

    wi = w_in[0]
    o = 3 * fox_w + nfh
    w_qkv = jnp.concatenate([wi[:, :fox_w], wi[:, o:o + fox_w],
                             wi[:, fox_w:2 * fox_w], wi[:, o + fox_w:o + 2 * fox_w],
                             wi[:, 2 * fox_w:3 * fox_w], wi[:, o + 2 * fox_w:]], axis=1).astype(BF16)
    w_fg_t = wi[:, 3 * fox_w:o].T
    b_fb = jnp.broadcast_to(b_f[0][:, None], (nfh, LANES))
    rope_c, rope_a, rope_b = _rope_tables(seq)

    q, k, v, f = _ln_inproj(x, mod, g_attn, w_qkv, w_fg_t, b_fb, rope_c, rope_a, rope_b)

    lam = jnp.concatenate([lambda_q1, lambda_k1, lambda_q2, lambda_k2], axis=0)
    mixed = _attention(q, k, v, f.reshape(bsz, nfh // 2, 2, seq), lam, g_subln)

    x1, h2 = _outproj(mixed, x, mod, w_o[0].astype(BF16), g_ffn)

    n = bsz * seq
    h2 = h2.reshape(n, d)
    cidx, cneg = _candidate_tables()
    r2, e2, n1, c1 = _route(h2, w_pq[0].T.astype(BF16), sub_keys[0].astype(BF16), cidx, cneg)

    n1 = n1.transpose(1, 0, 2)
    c1 = c1.transpose(1, 0, 2)
    out = _peer(h2, u_experts[0].astype(BF16), v_experts[0].T.astype(BF16), r2, e2, n1, c1,
                x1.reshape(n, d), mod, g_final.reshape(1, d), seq)
    return out.reshape(bsz, seq, d)
```
